```python
import jax, jax.numpy as jnp
from jax import lax
import numpy as np

D_MODEL = 2048
BATCH = 1
SEQ = 8192
DEPTH = 2
DEC_BATCH = 128
DEC_SEQ = 1
PAST_LEN = 2048
PAGE_SIZE = 128

N_HEADS = 8
HEAD_DIM = D_MODEL // 16
ATTN_WIDTH = N_HEADS * HEAD_DIM
LRU_WIDTH = D_MODEL // 2
LRU_BLOCKS = 8
LRU_BLOCK_DIM = LRU_WIDTH // LRU_BLOCKS
LRU_CONV_W = 4
LRU_C = 8.0
CONF_WIDTH = D_MODEL // 2
CONF_CONV_W = 31
D_FF = 5504
FFN_CONV_W = 3
N_BRANCH = 3
Q_BLOCK = 128
EPS = 1e-6
NEG_INF = -1e30
IN_SIZES = (ATTN_WIDTH, ATTN_WIDTH, ATTN_WIDTH, N_HEADS, LRU_WIDTH, LRU_WIDTH, 2 * CONF_WIDTH, D_MODEL, D_MODEL, D_MODEL)
IN_COLS = sum(IN_SIZES)

kernel_name = 'hybrid_fox_rglru_conformer_convffn_step'


def rmsnorm(x, g):
    xf = x.astype(jnp.float32)
    y = xf * lax.rsqrt(jnp.mean(xf * xf, axis=-1, keepdims=True) + EPS)
    return (y * g.astype(jnp.float32)).astype(x.dtype)


def layernorm(x, g, b):
    xf = x.astype(jnp.float32)
    mu = jnp.mean(xf, axis=-1, keepdims=True)
    xc = xf - mu
    var = jnp.mean(xc * xc, axis=-1, keepdims=True)
    return (xc * lax.rsqrt(var + EPS) * g.astype(jnp.float32) + b.astype(jnp.float32)).astype(x.dtype)


def split_cols(z, sizes):
    idx = np.cumsum(np.array(sizes))[:-1].tolist()
    return jnp.split(z, idx, axis=-1)


def causal_dwconv(x, buf, w, b):
    width = w.shape[0]
    xp = jnp.concatenate([buf.astype(x.dtype), x], axis=1)
    y = lax.conv_general_dilated(xp, w.astype(x.dtype)[:, None, :], window_strides=(1,), padding='VALID',
                                 dimension_numbers=('NWC', 'WIO', 'NWC'), feature_group_count=x.shape[-1])
    return y + b.astype(x.dtype), xp[:, xp.shape[1] - (width - 1):]


def linear_scan(a, b, h0):
    b = b.at[:, 0].add(a[:, 0] * h0)
    def combine(lhs, rhs):
        a1, b1 = lhs
        a2, b2 = rhs
        return a1 * a2, a2 * b1 + b2
    _, h = lax.associative_scan(combine, (a, b), axis=1)
    return h


def fox_attention(q, k, v, F, q_pos, k_pos):
    B, T, H, Dh = q.shape
    L = k.shape[1]
    blk = Q_BLOCK if T % Q_BLOCK == 0 else T
    nb = T // blk
    scale = Dh ** -0.5
    kf = k.astype(jnp.float32)
    vf = v.astype(jnp.float32)
    Fk = jnp.transpose(F, (0, 2, 1))[:, :, None, :]
    qb = jnp.moveaxis(q.astype(jnp.float32).reshape(B, nb, blk, H, Dh), 1, 0)
    Fqb = jnp.moveaxis(F[:, L - T:].reshape(B, nb, blk, H), 1, 0)
    pb = q_pos.reshape(nb, blk)

    def one_block(args):
        qi, Fi, pi = args
        s = jnp.einsum('bqhd,bkhd->bhqk', qi, kf) * scale
        s = s + jnp.transpose(Fi, (0, 2, 1))[..., None] - Fk
        s = jnp.where(k_pos[None, :] <= pi[:, None], s, NEG_INF)
        p = jax.nn.softmax(s, axis=-1)
        return jnp.einsum('bhqk,bkhd->bqhd', p, vf)

    o = lax.map(one_block, (qb, Fqb, pb))
    return jnp.moveaxis(o, 0, 1).reshape(B, T, H * Dh).astype(q.dtype)


def decoder_layer(x, c, past, W):
    (w_mod, b_mod, g_mix, g_ffn, w_in, b_forget, g_q, g_k, w_lconv, b_lconv, w_rg, b_rg, w_ig, b_ig, lru_L,
     w_cconv, b_cconv, cln_g, cln_b, w_o_attn, w_o_lru, w_o_conf, w_out, w_up, w_fconv, b_fconv, w_down) = W
    k_past, v_past, lf_past, h0, lbuf, cbuf, fbuf = past
    B, T, _ = x.shape
    f32 = jnp.float32

    mod = (jax.nn.silu(c) @ w_mod + b_mod)[:, None, :]
    shift1, scale1, gate1, shift2, scale2, gate2 = jnp.split(mod, 6, axis=-1)

    h = rmsnorm(x, g_mix) * (1.0 + scale1) + shift1
    q, k, v, fl, xl, gl, glu, gA, gB, gC = split_cols(h @ w_in, IN_SIZES)

    q = rmsnorm(q.reshape(B, T, N_HEADS, HEAD_DIM), g_q)
    k = rmsnorm(k.reshape(B, T, N_HEADS, HEAD_DIM), g_k)
    v = v.reshape(B, T, N_HEADS, HEAD_DIM)
    lf = jax.nn.log_sigmoid((fl + b_forget).astype(f32))
    k_all = jnp.concatenate([k_past.astype(k.dtype), k], axis=1)
    v_all = jnp.concatenate([v_past.astype(v.dtype), v], axis=1)
    lf_all = jnp.concatenate([lf_past.astype(f32), lf], axis=1)
    F = lax.cumsum(lf_all, axis=1)
    L = k_all.shape[1]
    k_pos = jnp.arange(L, dtype=jnp.int32)
    q_pos = k_pos[L - T:]
    o_attn = fox_attention(q, k_all, v_all, F, q_pos, k_pos) @ w_o_attn

    xc, new_lbuf = causal_dwconv(xl, lbuf, w_lconv, b_lconv)
    xb = xc.reshape(B, T, LRU_BLOCKS, LRU_BLOCK_DIM)
    r = jax.nn.sigmoid((jnp.einsum('btnd,nde->btne', xb, w_rg).reshape(B, T, LRU_WIDTH) + b_rg).astype(f32))
    i = jax.nn.sigmoid((jnp.einsum('btnd,nde->btne', xb, w_ig).reshape(B, T, LRU_WIDTH) + b_ig).astype(f32))
    log_a = -LRU_C * r * jax.nn.softplus(-lru_L.astype(f32))
    a = jnp.exp(log_a)
    mult = jnp.sqrt(-jnp.expm1(2.0 * log_a))
    hs = linear_scan(a, mult * i * xc.astype(f32), h0.astype(f32))
    o_lru = (hs.astype(x.dtype) * jax.nn.gelu(gl)) @ w_o_lru

    ga, gb = jnp.split(glu, 2, axis=-1)
    u = ga * jax.nn.sigmoid(gb)
    uc, new_cbuf = causal_dwconv(u, cbuf, w_cconv, b_cconv)
    o_conf = jax.nn.silu(layernorm(uc, cln_g, cln_b)) @ w_o_conf

    merged = jax.nn.sigmoid(gA) * o_attn + jax.nn.sigmoid(gB) * o_lru + jax.nn.sigmoid(gC) * o_conf
    x = x + gate1 * (merged @ w_out)

    h = rmsnorm(x, g_ffn) * (1.0 + scale2) + shift2
    ua, uv = jnp.split(h @ w_up, 2, axis=-1)
    uac, new_fbuf = causal_dwconv(ua, fbuf, w_fconv, b_fconv)
    x = x + gate2 * ((jax.nn.gelu(uac) * uv) @ w_down)

    return x, (k, v, lf, hs[:, -1].astype(x.dtype), new_lbuf, new_cbuf, new_fbuf)


def setup_inputs(seed: int = 0) -> dict:
    key = jax.random.key(seed)
    ks = iter(jax.random.split(key, 48))
    f32 = jnp.float32

    def nrm(shape, scale=1.0):
        return jax.random.normal(next(ks), shape, f32) * scale

    def unif(shape, lo, hi):
        return jax.random.uniform(next(ks), shape, f32, lo, hi)

    n_pages = PAST_LEN // PAGE_SIZE
    n_used = DEC_BATCH * n_pages
    n_pool = n_used + max(1, n_used // 4)
    page_table = jax.random.permutation(next(ks), n_pool)[:n_used].reshape(DEC_BATCH, n_pages).astype(jnp.int32)

    b_forget = unif((DEPTH, N_HEADS), 2.0, 7.0)
    u = unif((DEPTH, LRU_WIDTH), 0.9, 0.999)
    lru_L = jnp.log(u) - jnp.log1p(-u)

    x_prompt = nrm((BATCH, SEQ, D_MODEL))
    x_sample = nrm((DEC_BATCH, DEC_SEQ, D_MODEL))
    cache_k = nrm((DEPTH, n_pool, PAGE_SIZE, N_HEADS, HEAD_DIM))
    cache_v = nrm((DEPTH, n_pool, PAGE_SIZE, N_HEADS, HEAD_DIM))
    cache_logf = jax.nn.log_sigmoid(b_forget[:, None, None, :] + nrm((DEPTH, n_pool, PAGE_SIZE, N_HEADS), 0.5))
    state_lru_h = nrm((DEPTH, DEC_BATCH, LRU_WIDTH), 0.5)
    state_lru_conv = nrm((DEPTH, DEC_BATCH, LRU_CONV_W - 1, LRU_WIDTH))
    state_conf_conv = nrm((DEPTH, DEC_BATCH, CONF_CONV_W - 1, CONF_WIDTH))
    state_ffn_conv = nrm((DEPTH, DEC_BATCH, FFN_CONV_W - 1, D_FF))
    c_prompt = nrm((BATCH, D_MODEL))
    c_sample = nrm((DEC_BATCH, D_MODEL))

    w_mod = nrm((DEPTH, D_MODEL, 6 * D_MODEL), 0.5 * D_MODEL ** -0.5)
    b_mod = nrm((DEPTH, 6 * D_MODEL), 0.02)
    norm_mix = 1.0 + nrm((DEPTH, D_MODEL), 0.05)
    norm_ffn = 1.0 + nrm((DEPTH, D_MODEL), 0.05)
    w_in = nrm((DEPTH, D_MODEL, IN_COLS), D_MODEL ** -0.5)
    q_norm = 1.0 + nrm((DEPTH, HEAD_DIM), 0.05)
    k_norm = 1.0 + nrm((DEPTH, HEAD_DIM), 0.05)
    w_lru_conv = nrm((DEPTH, LRU_CONV_W, LRU_WIDTH), LRU_CONV_W ** -0.5)
    b_lru_conv = nrm((DEPTH, LRU_WIDTH), 0.02)
    w_rg = nrm((DEPTH, LRU_BLOCKS, LRU_BLOCK_DIM, LRU_BLOCK_DIM), LRU_BLOCK_DIM ** -0.5)
    b_rg = nrm((DEPTH, LRU_WIDTH), 0.02)
    w_ig = nrm((DEPTH, LRU_BLOCKS, LRU_BLOCK_DIM, LRU_BLOCK_DIM), LRU_BLOCK_DIM ** -0.5)
    b_ig = nrm((DEPTH, LRU_WIDTH), 0.02)
    w_conf_conv = nrm((DEPTH, CONF_CONV_W, CONF_WIDTH), CONF_CONV_W ** -0.5)
    b_conf_conv = nrm((DEPTH, CONF_WIDTH), 0.02)
    conf_ln_g = 1.0 + nrm((DEPTH, CONF_WIDTH), 0.05)
    conf_ln_b = nrm((DEPTH, CONF_WIDTH), 0.02)
    w_o_attn = nrm((DEPTH, ATTN_WIDTH, D_MODEL), ATTN_WIDTH ** -0.5)
    w_o_lru = nrm((DEPTH, LRU_WIDTH, D_MODEL), LRU_WIDTH ** -0.5)
    w_o_conf = nrm((DEPTH, CONF_WIDTH, D_MODEL), CONF_WIDTH ** -0.5)
    w_out = nrm((DEPTH, D_MODEL, D_MODEL), D_MODEL ** -0.5)
    w_up = nrm((DEPTH, D_MODEL, 2 * D_FF), D_MODEL ** -0.5)
    w_ffn_conv = nrm((DEPTH, FFN_CONV_W, D_FF), FFN_CONV_W ** -0.5)
    b_ffn_conv = nrm((DEPTH, D_FF), 0.02)
    w_down = nrm((DEPTH, D_FF, D_MODEL), D_FF ** -0.5)

    return {'x_prompt': x_prompt, 'x_sample': x_sample, 'cache_k': cache_k, 'cache_v': cache_v,
            'cache_logf': cache_logf, 'state_lru_h': state_lru_h, 'state_lru_conv': state_lru_conv,
            'state_conf_conv': state_conf_conv, 'state_ffn_conv': state_ffn_conv, 'page_table': page_table,
            'c_prompt': c_prompt, 'c_sample': c_sample, 'w_mod': w_mod, 'b_mod': b_mod,
            'norm_mix': norm_mix, 'norm_ffn': norm_ffn, 'w_in': w_in, 'b_forget': b_forget,
            'q_norm': q_norm, 'k_norm': k_norm, 'w_lru_conv': w_lru_conv, 'b_lru_conv': b_lru_conv,
            'w_rg': w_rg, 'b_rg': b_rg, 'w_ig': w_ig, 'b_ig': b_ig, 'lru_L': lru_L,
            'w_conf_conv': w_conf_conv, 'b_conf_conv': b_conf_conv, 'conf_ln_g': conf_ln_g,
            'conf_ln_b': conf_ln_b, 'w_o_attn': w_o_attn, 'w_o_lru': w_o_lru, 'w_o_conf': w_o_conf,
            'w_out': w_out, 'w_up': w_up, 'w_ffn_conv': w_ffn_conv, 'b_ffn_conv': b_ffn_conv,
            'w_down': w_down}


def reference(x_prompt, x_sample, cache_k, cache_v, cache_logf, state_lru_h, state_lru_conv, state_conf_conv,
              state_ffn_conv, page_table, c_prompt, c_sample, w_mod, b_mod, norm_mix, norm_ffn, w_in, b_forget,
              q_norm, k_norm, w_lru_conv, b_lru_conv, w_rg, b_rg, w_ig, b_ig, lru_L, w_conf_conv, b_conf_conv,
              conf_ln_g, conf_ln_b, w_o_attn, w_o_lru, w_o_conf, w_out, w_up, w_ffn_conv, b_ffn_conv, w_down):
    dt = x_prompt.dtype
    bp = x_prompt.shape[0]
    bs = x_sample.shape[0]
    xp, xs = x_prompt, x_sample
    new_p = [[] for _ in range(7)]
    new_s = [[] for _ in range(7)]
    for l in range(DEPTH):
        W = (w_mod[l], b_mod[l], norm_mix[l], norm_ffn[l], w_in[l], b_forget[l], q_norm[l], k_norm[l],
             w_lru_conv[l], b_lru_conv[l], w_rg[l], b_rg[l], w_ig[l], b_ig[l], lru_L[l], w_conf_conv[l],
             b_conf_conv[l], conf_ln_g[l], conf_ln_b[l], w_o_attn[l], w_o_lru[l], w_o_conf[l], w_out[l],
             w_up[l], w_ffn_conv[l], b_ffn_conv[l], w_down[l])
        past_p = (jnp.zeros((bp, 0, N_HEADS, HEAD_DIM), dt), jnp.zeros((bp, 0, N_HEADS, HEAD_DIM), dt),
                  jnp.zeros((bp, 0, N_HEADS), jnp.float32), jnp.zeros((bp, LRU_WIDTH), dt),
                  jnp.zeros((bp, LRU_CONV_W - 1, LRU_WIDTH), dt), jnp.zeros((bp, CONF_CONV_W - 1, CONF_WIDTH), dt),
                  jnp.zeros((bp, FFN_CONV_W - 1, D_FF), dt))
        k_past = cache_k[l][page_table].reshape(bs, -1, N_HEADS, HEAD_DIM)
        v_past = cache_v[l][page_table].reshape(bs, -1, N_HEADS, HEAD_DIM)
        lf_past = cache_logf[l][page_table].reshape(bs, -1, N_HEADS)
        past_s = (k_past, v_past, lf_past, state_lru_h[l], state_lru_conv[l], state_conf_conv[l], state_ffn_conv[l])
        xp, st_p = decoder_layer(xp, c_prompt, past_p, W)
        xs, st_s = decoder_layer(xs, c_sample, past_s, W)
        for j in range(7):
            new_p[j].append(st_p[j])
            new_s[j].append(st_s[j])
    k_p = jnp.stack(new_p[0])
    v_p = jnp.stack(new_p[1])
    lf_p = jnp.stack(new_p[2])
    h_p = jnp.stack(new_p[3])
    lb_p = jnp.stack(new_p[4])
    cb_p = jnp.stack(new_p[5])
    fb_p = jnp.stack(new_p[6])
    k_s = jnp.stack(new_s[0])
    v_s = jnp.stack(new_s[1])
    lf_s = jnp.stack(new_s[2])
    h_s = jnp.stack(new_s[3])
    lb_s = jnp.stack(new_s[4])
    cb_s = jnp.stack(new_s[5])
    fb_s = jnp.stack(new_s[6])
    return (xp, xs, k_p, v_p, lf_p, h_p, lb_p, cb_p, fb_p, k_s, v_s, lf_s, h_s, lb_s, cb_s, fb_s)
```

```python
import functools

import jax
import jax.numpy as jnp
from jax import lax
from jax.experimental import pallas as pl
from jax.experimental.pallas import tpu as pltpu

F32 = jnp.float32
BF16 = jnp.bfloat16

N_HEADS = 8
LRU_BLOCKS = 8
LRU_C = 8.0
EPS = 1e-6
NEG_INF = -1e30
LANE = 128
SUBLANE = 8
VMEM_LIMIT = 56 * 1024 * 1024


def _cp(*sem):
    return pltpu.CompilerParams(dimension_semantics=sem, vmem_limit_bytes=VMEM_LIMIT)


def _row_tile(m, pref):
    t = min(pref, m)
    assert m % t == 0, (m, t)
    return t


def _split3(x):
    hi = x.astype(BF16)
    r1 = x - hi.astype(F32)
    mid = r1.astype(BF16)
    lo = (r1 - mid.astype(F32)).astype(BF16)
    return hi, mid, lo


def _mod_kernel(c_ref, w_ref, b_ref, o_ref):
    c = c_ref[...]
    a = (c * jax.nn.sigmoid(c)).astype(BF16)
    o_ref[...] = jnp.dot(a, w_ref[...].astype(BF16), preferred_element_type=F32) + b_ref[...]


def _modulation(c_all, w_mod, b_mod):
    depth, d, n = w_mod.shape
    mc = c_all.shape[0]
    tn = 1024
    return pl.pallas_call(
        _mod_kernel,
        grid=(depth, n // tn),
        in_specs=[pl.BlockSpec((mc, d), lambda l, j: (0, 0)),
                  pl.BlockSpec((None, d, tn), lambda l, j: (l, 0, j)),
                  pl.BlockSpec((None, 1, tn), lambda l, j: (l, 0, j))],
        out_specs=pl.BlockSpec((None, mc, tn), lambda l, j: (l, 0, j)),
        out_shape=jax.ShapeDtypeStruct((depth, mc, n), F32),
        compiler_params=_cp("arbitrary", "arbitrary"),
        name="modulation",
    )(c_all, w_mod, b_mod)


def _mod_spec(mod, tm, tn, row_of, col_of):
    if mod.shape[0] == 1:
        return pl.BlockSpec((1, tn), lambda *g: (0, col_of(*g)))
    return pl.BlockSpec((tm, tn), lambda *g: (row_of(*g), col_of(*g)))


def _norm_kernel(x_ref, g_ref, sc_ref, sh_ref, *rest, with_forget):
    x = x_ref[...]
    ms = jnp.mean(x * x, axis=-1, keepdims=True)
    y = x * lax.rsqrt(ms + EPS) * g_ref[...]
    h = (y * (1.0 + sc_ref[...]) + sh_ref[...]).astype(BF16)
    if with_forget:
        wfl_ref, bf_ref, h_ref, lf_ref = rest
        fl = jnp.dot(h, wfl_ref[...], preferred_element_type=F32) + bf_ref[...]
        lf_ref[...] = jax.nn.log_sigmoid(fl)
    else:
        (h_ref,) = rest
    h_ref[...] = h


def _norm_mod(x, g, scale, shift, w_fl=None, b_fl=None):
    m, d = x.shape
    tm = _row_tile(m, 256)
    with_forget = w_fl is not None
    row = lambda i: i
    col = lambda i: 0
    in_specs = [pl.BlockSpec((tm, d), lambda i: (i, 0)),
                pl.BlockSpec((1, d), lambda i: (0, 0)),
                _mod_spec(scale, tm, d, row, col),
                _mod_spec(shift, tm, d, row, col)]
    args = [x, g, scale, shift]
    out_specs = [pl.BlockSpec((tm, d), lambda i: (i, 0))]
    out_shape = [jax.ShapeDtypeStruct((m, d), BF16)]
    if with_forget:
        in_specs += [pl.BlockSpec((d, LANE), lambda i: (0, 0)), pl.BlockSpec((1, LANE), lambda i: (0, 0))]
        args += [w_fl, b_fl]
        out_specs.append(pl.BlockSpec((tm, LANE), lambda i: (i, 0)))
        out_shape.append(jax.ShapeDtypeStruct((m, LANE), F32))
    out = pl.pallas_call(
        functools.partial(_norm_kernel, with_forget=with_forget),
        grid=(m // tm,), in_specs=in_specs, out_specs=out_specs, out_shape=out_shape,
        compiler_params=_cp("arbitrary"), name="norm_mod",
    )(*args)
    return out if with_forget else out[0]


def _proj_kernel(h_ref, w_ref, *rest, head_norm, gain_scale, act):
    y = jnp.dot(h_ref[...], w_ref[...], preferred_element_type=F32)
    if head_norm:
        g_ref, *out_refs = rest
        g = g_ref[...] * gain_scale
        hd = g.shape[-1]
        for hh in range(y.shape[-1] // hd):
            yh = y[:, hh * hd:(hh + 1) * hd]
            ms = jnp.mean(yh * yh, axis=-1, keepdims=True)
            yn = yh * lax.rsqrt(ms + EPS) * g
            for o in out_refs:
                o[:, hh * hd:(hh + 1) * hd] = yn.astype(o.dtype)
    else:
        out_refs = rest
        if act == "gelu":
            y = jax.nn.gelu(y)
        for o in out_refs:
            o[...] = y.astype(o.dtype)


def _proj(h, w, out_dtypes, gain=None, gain_scale=1.0, act=None):
    m, d = h.shape
    n = w.shape[1]
    tm = _row_tile(m, 512)
    in_specs = [pl.BlockSpec((tm, d), lambda i: (i, 0)), pl.BlockSpec((d, n), lambda i: (0, 0))]
    args = [h, w]
    if gain is not None:
        in_specs.append(pl.BlockSpec(gain.shape, lambda i: (0, 0)))
        args.append(gain)
    out = pl.pallas_call(
        functools.partial(_proj_kernel, head_norm=gain is not None, gain_scale=gain_scale, act=act),
        grid=(m // tm,), in_specs=in_specs,
        out_specs=[pl.BlockSpec((tm, n), lambda i: (i, 0)) for _ in out_dtypes],
        out_shape=[jax.ShapeDtypeStruct((m, n), dt) for dt in out_dtypes],
        compiler_params=_cp("arbitrary"), name="proj",
    )(*args)
    return out


def _glu_kernel(h_ref, wa_ref, wb_ref, u_ref):
    h = h_ref[...]
    ga = jnp.dot(h, wa_ref[...], preferred_element_type=F32)
    gb = jnp.dot(h, wb_ref[...], preferred_element_type=F32)
    u_ref[...] = ga * jax.nn.sigmoid(gb)


def _glu(h, wa, wb):
    m, d = h.shape
    n = wa.shape[1]
    tm = _row_tile(m, 512)
    return pl.pallas_call(
        _glu_kernel, grid=(m // tm,),
        in_specs=[pl.BlockSpec((tm, d), lambda i: (i, 0)), pl.BlockSpec((d, n), lambda i: (0, 0)),
                  pl.BlockSpec((d, n), lambda i: (0, 0))],
        out_specs=pl.BlockSpec((tm, n), lambda i: (i, 0)),
        out_shape=jax.ShapeDtypeStruct((m, n), F32),
        compiler_params=_cp("arbitrary"), name="glu",
    )(h, wa, wb)


def _cumsum_kernel(x_ref, o_ref, carry_ref):
    @pl.when(pl.program_id(0) == 0)
    def _():
        carry_ref[...] = jnp.zeros_like(carry_ref)

    x = x_ref[...]
    tk = x.shape[-1]
    r = lax.broadcasted_iota(jnp.int32, (tk, tk), 0)
    c = lax.broadcasted_iota(jnp.int32, (tk, tk), 1)
    tri = jnp.where(r <= c, 1.0, 0.0).astype(BF16)
    parts = jnp.concatenate(_split3(x), axis=0)
    cs = jnp.dot(parts, tri, preferred_element_type=F32)
    nh = x.shape[0]
    out = cs[0:nh] + cs[nh:2 * nh] + cs[2 * nh:3 * nh] + carry_ref[...]
    o_ref[...] = out
    carry_ref[...] = out[:, tk - 1:tk]


def _cumsum_lanes(x):
    nh, t = x.shape
    tk = _row_tile(t, 512)
    return pl.pallas_call(
        _cumsum_kernel, grid=(t // tk,),
        in_specs=[pl.BlockSpec((nh, tk), lambda j: (0, j))],
        out_specs=pl.BlockSpec((nh, tk), lambda j: (0, j)),
        out_shape=jax.ShapeDtypeStruct((nh, t), F32),
        scratch_shapes=[pltpu.VMEM((nh, 1), F32)],
        compiler_params=_cp("arbitrary"), name="cumsum_logf",
    )(x)


def _attn_kernel(q_ref, k_ref, v_ref, f_ref, o_ref, m_ref, l_ref, acc_ref):
    i = pl.program_id(1)
    j = pl.program_id(2)
    nk = pl.num_programs(2)

    @pl.when(j == 0)
    def _():
        m_ref[...] = jnp.full_like(m_ref, NEG_INF)
        l_ref[...] = jnp.zeros_like(l_ref)
        acc_ref[...] = jnp.zeros_like(acc_ref)

    def step(masked):
        s = lax.dot_general(q_ref[...], k_ref[...], (((1,), (1,)), ((), ())), preferred_element_type=F32)
        s = s - f_ref[...]
        if masked:
            r = lax.broadcasted_iota(jnp.int32, s.shape, 0)
            c = lax.broadcasted_iota(jnp.int32, s.shape, 1)
            s = jnp.where(c <= r, s, NEG_INF)
        m_old = m_ref[...]
        m_new = jnp.maximum(m_old, jnp.max(s, axis=-1, keepdims=True))
        alpha = jnp.exp(m_old - m_new)
        p = jnp.exp(s - m_new)
        l_ref[...] = alpha * l_ref[...] + jnp.sum(p, axis=-1, keepdims=True)
        acc_ref[...] = alpha * acc_ref[...] + jnp.dot(p.astype(BF16), v_ref[...], preferred_element_type=F32)
        m_ref[...] = m_new

    @pl.when(j < i)
    def _():
        step(False)

    @pl.when(j == i)
    def _():
        step(True)

    @pl.when(j == nk - 1)
    def _():
        o_ref[...] = (acc_ref[...] / l_ref[...]).astype(o_ref.dtype)


def _attention_prompt(q, k, v, f_t):
    t, aw = q.shape
    hd = aw // N_HEADS
    tq = _row_tile(t, 512)
    nq = t // tq
    kv_map = lambda h, i, j: (jnp.minimum(j, i), h)
    return pl.pallas_call(
        _attn_kernel, grid=(N_HEADS, nq, nq),
        in_specs=[pl.BlockSpec((tq, hd), lambda h, i, j: (i, h)),
                  pl.BlockSpec((tq, hd), kv_map),
                  pl.BlockSpec((tq, hd), kv_map),
                  pl.BlockSpec((None, 1, tq), lambda h, i, j: (h, 0, jnp.minimum(j, i)))],
        out_specs=pl.BlockSpec((tq, hd), lambda h, i, j: (i, h)),
        out_shape=jax.ShapeDtypeStruct((t, aw), BF16),
        scratch_shapes=[pltpu.VMEM((tq, 1), F32), pltpu.VMEM((tq, 1), F32), pltpu.VMEM((tq, hd), F32)],
        compiler_params=_cp("arbitrary", "arbitrary", "arbitrary"), name="attention_prompt",
    )(q, k, v, f_t)


def _decode_kernel(pt_ref, q_ref, kn_ref, vn_ref, lfn_ref, kc_ref, vc_ref, lfc_ref, o_ref,
                   qbd_ref, m_ref, l_ref, acc_ref, fc_ref):
    del pt_ref
    p_idx = pl.program_id(1)
    n_pages = pl.num_programs(1)
    nh, aw = qbd_ref.shape
    hd = aw // nh
    head_of_lane = lax.broadcasted_iota(jnp.int32, (nh, aw), 1) // hd
    head_of_row = lax.broadcasted_iota(jnp.int32, (nh, aw), 0)
    diag = head_of_lane == head_of_row

    @pl.when(p_idx == 0)
    def _():
        qb = jnp.broadcast_to(q_ref[...].astype(F32), (nh, aw))
        qbd_ref[...] = jnp.where(diag, qb, 0.0).astype(qbd_ref.dtype)
        m_ref[...] = jnp.full_like(m_ref, NEG_INF)
        l_ref[...] = jnp.zeros_like(l_ref)
        acc_ref[...] = jnp.zeros_like(acc_ref)
        fc_ref[...] = jnp.zeros_like(fc_ref)

    lf = lfc_ref[...]
    pg = lf.shape[-1]
    r = lax.broadcasted_iota(jnp.int32, (pg, pg), 0)
    c = lax.broadcasted_iota(jnp.int32, (pg, pg), 1)
    tri = jnp.where(r <= c, 1.0, 0.0).astype(BF16)
    cs = jnp.dot(jnp.concatenate(_split3(lf), axis=0), tri, preferred_element_type=F32)
    f_page = cs[0:nh] + cs[nh:2 * nh] + cs[2 * nh:3 * nh] + fc_ref[...]
    fc_ref[...] = f_page[:, pg - 1:pg]

    kb = kc_ref[...].astype(BF16)
    s = lax.dot_general(qbd_ref[...], kb, (((1,), (1,)), ((), ())), preferred_element_type=F32)
    s = s - f_page
    m_old = m_ref[...]
    m_new = jnp.maximum(m_old, jnp.max(s, axis=-1, keepdims=True))
    alpha = jnp.exp(m_old - m_new)
    p = jnp.exp(s - m_new)
    l_ref[...] = alpha * l_ref[...] + jnp.sum(p, axis=-1, keepdims=True)
    acc_ref[...] = alpha * acc_ref[...] + jnp.dot(p.astype(BF16), vc_ref[...].astype(BF16),
                                                  preferred_element_type=F32)
    m_ref[...] = m_new

    @pl.when(p_idx == n_pages - 1)
    def _():
        f_new = fc_ref[...] + lfn_ref[...]
        kn = kn_ref[...].astype(BF16).astype(F32)
        s_new = jnp.sum(qbd_ref[...].astype(F32) * kn, axis=-1, keepdims=True) - f_new
        m_old = m_ref[...]
        m_new = jnp.maximum(m_old, s_new)
        alpha = jnp.exp(m_old - m_new)
        p_new = jnp.exp(s_new - m_new)
        l_fin = alpha * l_ref[...] + p_new
        vn = vn_ref[...].astype(BF16).astype(F32)
        acc = alpha * acc_ref[...] + p_new.astype(BF16).astype(F32) * vn
        o_full = acc / l_fin
        o_ref[...] = jnp.sum(jnp.where(diag, o_full, 0.0), axis=0, keepdims=True).astype(o_ref.dtype)


def _attention_sample(layer, q, k_new, v_new, lf_new, cache_k, cache_v, cache_lf_t, page_table):
    b, _, aw = q.shape
    n_pages = page_table.shape[1]
    page = cache_k.shape[2]
    row = lambda bi, p, pt: (bi, 0, 0)
    pg = lambda bi, p, pt: (layer, pt[bi, p], 0, 0)
    grid_spec = pltpu.PrefetchScalarGridSpec(
        num_scalar_prefetch=1, grid=(b, n_pages),
        in_specs=[pl.BlockSpec((None, 1, aw), row), pl.BlockSpec((None, 1, aw), row),
                  pl.BlockSpec((None, 1, aw), row), pl.BlockSpec((None, N_HEADS, 1), row),
                  pl.BlockSpec((None, None, page, aw), pg), pl.BlockSpec((None, None, page, aw), pg),
                  pl.BlockSpec((None, None, N_HEADS, page), pg)],
        out_specs=pl.BlockSpec((None, 1, aw), row),
        scratch_shapes=[pltpu.VMEM((N_HEADS, aw), BF16), pltpu.VMEM((N_HEADS, 1), F32),
                        pltpu.VMEM((N_HEADS, 1), F32), pltpu.VMEM((N_HEADS, aw), F32),
                        pltpu.VMEM((N_HEADS, 1), F32)])
    return pl.pallas_call(
        _decode_kernel, grid_spec=grid_spec,
        out_shape=jax.ShapeDtypeStruct((b, 1, aw), BF16),
        compiler_params=_cp("arbitrary", "arbitrary"), name="attention_sample",
    )(page_table, q, k_new, v_new, lf_new, cache_k, cache_v, cache_lf_t)


def _shift_rows(x, s, prev):
    rolled = pltpu.roll(x, s, axis=0)
    head = rolled[:SUBLANE]
    row = lax.broadcasted_iota(jnp.int32, head.shape, 0)
    head = jnp.where(row < s, pltpu.roll(prev, s, axis=0), head)
    return jnp.concatenate([head, rolled[SUBLANE:]], axis=0)


def _lru_gates(xc, wrg_ref, brg_ref, wig_ref, big_ref, lam_ref):
    xb = xc.astype(BF16)
    nb, bd, _ = wrg_ref.shape
    rg = jnp.concatenate([jnp.dot(xb[:, n * bd:(n + 1) * bd], wrg_ref[n], preferred_element_type=F32)
                          for n in range(nb)], axis=-1)
    ig = jnp.concatenate([jnp.dot(xb[:, n * bd:(n + 1) * bd], wig_ref[n], preferred_element_type=F32)
                          for n in range(nb)], axis=-1)
    r = jax.nn.sigmoid(rg + brg_ref[...])
    i = jax.nn.sigmoid(ig + big_ref[...])
    log_a = -LRU_C * r * jax.nn.softplus(-lam_ref[...])
    a = jnp.exp(log_a)
    mult = jnp.sqrt(-jnp.tanh(log_a) * (1.0 + a * a))
    return a, mult * i * xc


def _lru_seq_kernel(xl_ref, gg_ref, wc_ref, bc_ref, wrg_ref, brg_ref, wig_ref, big_ref, lam_ref,
                    o_ref, hl_ref, prev_ref, h_ref):
    @pl.when(pl.program_id(0) == 0)
    def _():
        prev_ref[...] = jnp.zeros_like(prev_ref)
        h_ref[...] = jnp.zeros_like(h_ref)

    x = xl_ref[...]
    tm = x.shape[0]
    prev = prev_ref[...]
    width = wc_ref.shape[0]
    xc = x * wc_ref[width - 1:width, :] + bc_ref[...]
    for s in range(1, width):
        xc = xc + _shift_rows(x, s, prev) * wc_ref[width - 1 - s:width - s, :]
    prev_ref[...] = x[tm - SUBLANE:]

    a, b = _lru_gates(xc, wrg_ref, brg_ref, wig_ref, big_ref, lam_ref)
    row = lax.broadcasted_iota(jnp.int32, a.shape, 0)
    s = 1
    while s < tm:
        keep = row >= s
        a_s = jnp.where(keep, pltpu.roll(a, s, axis=0), 1.0)
        b_s = jnp.where(keep, pltpu.roll(b, s, axis=0), 0.0)
        b = a * b_s + b
        a = a * a_s
        s *= 2
    h = a * h_ref[...] + b
    h_last = h[tm - 1:tm]
    h_ref[...] = h_last
    hl_ref[...] = h_last
    o_ref[...] = (h * gg_ref[...].astype(F32)).astype(o_ref.dtype)


def _lru_seq(xl, gg, wc, bc, wrg, brg, wig, big, lam):
    t, w = xl.shape
    tm = _row_tile(t, 256)
    full = lambda a: pl.BlockSpec(a.shape, lambda i: (0,) * a.ndim)
    return pl.pallas_call(
        _lru_seq_kernel, grid=(t // tm,),
        in_specs=[pl.BlockSpec((tm, w), lambda i: (i, 0)), pl.BlockSpec((tm, w), lambda i: (i, 0)),
                  full(wc), full(bc), full(wrg), full(brg), full(wig), full(big), full(lam)],
        out_specs=[pl.BlockSpec((tm, w), lambda i: (i, 0)), pl.BlockSpec((1, w), lambda i: (0, 0))],
        out_shape=[jax.ShapeDtypeStruct((t, w), BF16), jax.ShapeDtypeStruct((1, w), F32)],
        scratch_shapes=[pltpu.VMEM((SUBLANE, w), F32), pltpu.VMEM((1, w), F32)],
        compiler_params=_cp("arbitrary"), name="lru_seq",
    )(xl, gg, wc, bc, wrg, brg, wig, big, lam)


def _lru_step_kernel(xl_ref, gg_ref, buf_ref, h0_ref, wc_ref, bc_ref, wrg_ref, brg_ref, wig_ref, big_ref, lam_ref,
                     o_ref, h_ref):
    width = wc_ref.shape[0]
    xc = xl_ref[...] * wc_ref[width - 1:width, :] + bc_ref[...]
    for j in range(width - 1):
        xc = xc + buf_ref[j] * wc_ref[j:j + 1, :]
    a, b = _lru_gates(xc, wrg_ref, brg_ref, wig_ref, big_ref, lam_ref)
    h = a * h0_ref[...] + b
    h_ref[...] = h
    o_ref[...] = (h * gg_ref[...].astype(F32)).astype(o_ref.dtype)


def _lru_step(xl, gg, buf_t, h0, wc, bc, wrg, brg, wig, big, lam):
    b, w = xl.shape
    full = lambda a: pl.BlockSpec(a.shape, lambda i: (0,) * a.ndim)
    args = (xl, gg, buf_t, h0, wc, bc, wrg, brg, wig, big, lam)
    return pl.pallas_call(
        _lru_step_kernel, grid=(1,), in_specs=[full(a) for a in args],
        out_specs=[pl.BlockSpec((b, w), lambda i: (0, 0)), pl.BlockSpec((b, w), lambda i: (0, 0))],
        out_shape=[jax.ShapeDtypeStruct((b, w), BF16), jax.ShapeDtypeStruct((b, w), F32)],
        compiler_params=_cp("arbitrary"), name="lru_step",
    )(*args)


def _ln_silu(uc, g_ref, b_ref):
    mu = jnp.mean(uc, axis=-1, keepdims=True)
    xc = uc - mu
    var = jnp.mean(xc * xc, axis=-1, keepdims=True)
    y = xc * lax.rsqrt(var + EPS) * g_ref[...] + b_ref[...]
    return y * jax.nn.sigmoid(y)


def _conf_seq_kernel(u_ref, wc_ref, bc_ref, g_ref, b_ref, o_ref, ext_ref, *, halo):
    tm = u_ref.shape[0]
    width = wc_ref.shape[0]

    @pl.when(pl.program_id(0) == 0)
    def _():
        ext_ref[0:halo, :] = jnp.zeros((halo, ext_ref.shape[1]), F32)

    ext_ref[halo:halo + tm, :] = u_ref[...]
    base = halo - (width - 1)
    uc = ext_ref[pl.ds(base, tm), :] * wc_ref[0:1, :] + bc_ref[...]
    for j in range(1, width):
        uc = uc + ext_ref[pl.ds(base + j, tm), :] * wc_ref[j:j + 1, :]
    ext_ref[0:halo, :] = ext_ref[tm:tm + halo, :]
    o_ref[...] = _ln_silu(uc, g_ref, b_ref).astype(o_ref.dtype)


def _conf_seq(u, wc, bc, g, b):
    t, w = u.shape
    tm = _row_tile(t, 256)
    halo = -(-(wc.shape[0] - 1) // SUBLANE) * SUBLANE
    full = lambda a: pl.BlockSpec(a.shape, lambda i: (0,) * a.ndim)
    return pl.pallas_call(
        functools.partial(_conf_seq_kernel, halo=halo), grid=(t // tm,),
        in_specs=[pl.BlockSpec((tm, w), lambda i: (i, 0)), full(wc), full(bc), full(g), full(b)],
        out_specs=pl.BlockSpec((tm, w), lambda i: (i, 0)),
        out_shape=jax.ShapeDtypeStruct((t, w), BF16),
        scratch_shapes=[pltpu.VMEM((tm + halo, w), F32)],
        compiler_params=_cp("arbitrary"), name="conf_seq",
    )(u, wc, bc, g, b)


def _conf_step_kernel(u_ref, buf_ref, wc_ref, bc_ref, g_ref, b_ref, o_ref):
    width = wc_ref.shape[0]
    uc = u_ref[...] * wc_ref[width - 1:width, :] + bc_ref[...]
    for j in range(width - 1):
        uc = uc + buf_ref[j] * wc_ref[j:j + 1, :]
    o_ref[...] = _ln_silu(uc, g_ref, b_ref).astype(o_ref.dtype)


def _conf_step(u, buf_t, wc, bc, g, b):
    bsz, w = u.shape
    full = lambda a: pl.BlockSpec(a.shape, lambda i: (0,) * a.ndim)
    args = (u, buf_t, wc, bc, g, b)
    return pl.pallas_call(
        _conf_step_kernel, grid=(1,), in_specs=[full(a) for a in args],
        out_specs=pl.BlockSpec((bsz, w), lambda i: (0, 0)),
        out_shape=jax.ShapeDtypeStruct((bsz, w), BF16),
        compiler_params=_cp("arbitrary"), name="conf_step",
    )(*args)


def _merge_kernel(h_ref, a_ref, b_ref, c_ref, wga_ref, wgb_ref, wgc_ref, woa_ref, wob_ref, woc_ref, o_ref):
    h = h_ref[...]
    dot = lambda x, w: jnp.dot(x, w[...], preferred_element_type=F32)
    out = jax.nn.sigmoid(dot(h, wga_ref)) * dot(a_ref[...], woa_ref)
    out = out + jax.nn.sigmoid(dot(h, wgb_ref)) * dot(b_ref[...], wob_ref)
    out = out + jax.nn.sigmoid(dot(h, wgc_ref)) * dot(c_ref[...], woc_ref)
    o_ref[...] = out.astype(o_ref.dtype)


def _merge(h, a, b, c, wga, wgb, wgc, woa, wob, woc):
    m, d = h.shape
    w = a.shape[1]
    n = wga.shape[1]
    tm = _row_tile(m, 512)
    tn = 512
    act = lambda k: pl.BlockSpec((tm, k), lambda j, i: (i, 0))
    wgt = lambda k: pl.BlockSpec((k, tn), lambda j, i: (0, j))
    return pl.pallas_call(
        _merge_kernel, grid=(n // tn, m // tm),
        in_specs=[act(d), act(w), act(w), act(w), wgt(d), wgt(d), wgt(d), wgt(w), wgt(w), wgt(w)],
        out_specs=pl.BlockSpec((tm, tn), lambda j, i: (i, j)),
        out_shape=jax.ShapeDtypeStruct((m, n), BF16),
        compiler_params=_cp("arbitrary", "arbitrary"), name="merge",
    )(h, a, b, c, wga, wgb, wgc, woa, wob, woc)


def _res_kernel(x_ref, gate_ref, a_ref, w_ref, o_ref):
    y = jnp.dot(a_ref[...], w_ref[...], preferred_element_type=F32)
    o_ref[...] = x_ref[...] + gate_ref[...] * y


def _gated_residual(x, gate, a, w):
    m, n = x.shape
    k = a.shape[1]
    tm = _row_tile(m, 512)
    tn = 512
    return pl.pallas_call(
        _res_kernel, grid=(n // tn, m // tm),
        in_specs=[pl.BlockSpec((tm, tn), lambda j, i: (i, j)),
                  _mod_spec(gate, tm, tn, lambda j, i: i, lambda j, i: j),
                  pl.BlockSpec((tm, k), lambda j, i: (i, 0)),
                  pl.BlockSpec((k, tn), lambda j, i: (0, j))],
        out_specs=pl.BlockSpec((tm, tn), lambda j, i: (i, j)),
        out_shape=jax.ShapeDtypeStruct((m, n), F32),
        compiler_params=_cp("arbitrary", "arbitrary"), name="gated_residual",
    )(x, gate, a, w)


def _ffn_up_kernel(h_ref, wa_ref, wv_ref, wc_ref, bc_ref, *rest, seq):
    h = h_ref[...]
    ua = jnp.dot(h, wa_ref[...], preferred_element_type=F32)
    uv = jnp.dot(h, wv_ref[...], preferred_element_type=F32)
    width = wc_ref.shape[0]
    tm = ua.shape[0]
    uac = ua * wc_ref[width - 1:width, :] + bc_ref[...]
    if seq:
        g_ref, tail_ref, prev_ref = rest

        @pl.when(pl.program_id(1) == 0)
        def _():
            prev_ref[...] = jnp.zeros_like(prev_ref)

        prev = prev_ref[...]
        for s in range(1, width):
            uac = uac + _shift_rows(ua, s, prev) * wc_ref[width - 1 - s:width - s, :]
        tail = ua[tm - SUBLANE:]
        prev_ref[...] = tail
        tail_ref[...] = tail
    else:
        buf_ref, g_ref, ua_ref = rest
        for j in range(width - 1):
            uac = uac + buf_ref[j] * wc_ref[j:j + 1, :]
        ua_ref[...] = ua
    g_ref[...] = (jax.nn.gelu(uac) * uv).astype(g_ref.dtype)


def _ffn_up(h, wa, wv, wc, bc, buf_t=None):
    m, d = h.shape
    f = wa.shape[1]
    seq = buf_t is None
    tm = _row_tile(m, 512)
    tn = 512
    nj = pl.cdiv(f, tn)
    in_specs = [pl.BlockSpec((tm, d), lambda j, i: (i, 0)),
                pl.BlockSpec((d, tn), lambda j, i: (0, j)), pl.BlockSpec((d, tn), lambda j, i: (0, j)),
                pl.BlockSpec((wc.shape[0], tn), lambda j, i: (0, j)), pl.BlockSpec((1, tn), lambda j, i: (0, j))]
    args = [h, wa, wv, wc, bc]
    out_specs = [pl.BlockSpec((tm, tn), lambda j, i: (i, j))]
    out_shape = [jax.ShapeDtypeStruct((m, f), BF16)]
    scratch = []
    if seq:
        out_specs.append(pl.BlockSpec((SUBLANE, tn), lambda j, i: (0, j)))
        out_shape.append(jax.ShapeDtypeStruct((SUBLANE, f), F32))
        scratch.append(pltpu.VMEM((SUBLANE, tn), F32))
    else:
        in_specs.append(pl.BlockSpec((buf_t.shape[0], tm, tn), lambda j, i: (0, i, j)))
        args.append(buf_t)
        out_specs.append(pl.BlockSpec((tm, tn), lambda j, i: (i, j)))
        out_shape.append(jax.ShapeDtypeStruct((m, f), F32))
    return pl.pallas_call(
        functools.partial(_ffn_up_kernel, seq=seq), grid=(nj, m // tm),
        in_specs=in_specs, out_specs=out_specs, out_shape=out_shape, scratch_shapes=scratch,
        compiler_params=_cp("arbitrary", "arbitrary"), name="ffn_up",
    )(*args)


def _layer(x, mod, wts, state):
    d = x.shape[1]
    shift1, scale1, gate1, shift2, scale2, gate2 = [mod[:, n * d:(n + 1) * d] for n in range(6)]
    aw = wts["w_q"].shape[1]
    hd = aw // N_HEADS

    h, lf = _norm_mod(x, wts["g_mix"], scale1, shift1, wts["w_fl"], wts["b_fl"])
    (q,) = _proj(h, wts["w_q"], (BF16,), gain=wts["g_q"], gain_scale=hd ** -0.5)
    k, kb = _proj(h, wts["w_k"], (F32, BF16), gain=wts["g_k"])
    v, vb = _proj(h, wts["w_v"], (F32, BF16))
    (xl,) = _proj(h, wts["w_xl"], (F32,))
    (gg,) = _proj(h, wts["w_gl"], (BF16,), act="gelu")
    u = _glu(h, wts["w_ga"], wts["w_gb"])
    lf8 = lf[:, :N_HEADS]
    lru_args = (wts["w_lconv"], wts["b_lconv"], wts["w_rg"], wts["b_rg"], wts["w_ig"], wts["b_ig"], wts["lru_L"])
    conf_args = (wts["w_cconv"], wts["b_cconv"], wts["cln_g"], wts["cln_b"])

    if state is None:
        f_t = _cumsum_lanes(lf8.T)
        o_attn = _attention_prompt(q, kb, vb, f_t[:, None, :])
        o_lru, h_last = _lru_seq(xl, gg, *lru_args)
        o_conf = _conf_seq(u, *conf_args)
    else:
        m = x.shape[0]
        o_attn = _attention_sample(state["layer"], q[:, None, :], k[:, None, :], v[:, None, :], lf8[:, :, None],
                                   state["cache_k"], state["cache_v"], state["cache_lf_t"],
                                   state["page_table"]).reshape(m, aw)
        o_lru, h_last = _lru_step(xl, gg, state["lbuf_t"], state["h0"], *lru_args)
        o_conf = _conf_step(u, state["cbuf_t"], *conf_args)

    merged = _merge(h, o_attn, o_lru, o_conf, wts["w_gA"], wts["w_gB"], wts["w_gC"],
                    wts["w_o_attn"], wts["w_o_lru"], wts["w_o_conf"])
    x = _gated_residual(x, gate1, merged, wts["w_out"])

    h2 = _norm_mod(x, wts["g_ffn"], scale2, shift2)
    g, ua = _ffn_up(h2, wts["w_ua"], wts["w_uv"], wts["w_fconv"], wts["b_fconv"],
                    None if state is None else state["fbuf_t"])
    x = _gated_residual(x, gate2, g, wts["w_down"])
    return x, dict(k=k, v=v, lf=lf8, h=h_last, xl=xl, u=u, ua=ua)


def kernel(x_prompt, x_sample, cache_k, cache_v, cache_logf, state_lru_h, state_lru_conv, state_conf_conv, state_ffn_conv, page_table, c_prompt, c_sample, w_mod, b_mod, norm_mix, norm_ffn, w_in, b_forget, q_norm, k_norm, w_lru_conv, b_lru_conv, w_rg, b_rg, w_ig, b_ig, lru_L, w_conf_conv, b_conf_conv, conf_ln_g, conf_ln_b, w_o_attn, w_o_lru, w_o_conf, w_out, w_up, w_ffn_conv, b_ffn_conv, w_down):
    depth, d, _ = w_mod.shape
    bp, t, _ = x_prompt.shape
    bs = x_sample.shape[0]
    assert bp == 1 and x_sample.shape[1] == 1
    nh = N_HEADS
    hd = q_norm.shape[1]
    aw = nh * hd
    lw = w_lru_conv.shape[2]
    cw = w_conf_conv.shape[2]
    dff = w_ffn_conv.shape[2]
    n_pool, page = cache_k.shape[1], cache_k.shape[2]
    row = lambda a: a[:, None, :]

    c_all = jnp.concatenate([c_sample, c_prompt, jnp.zeros((SUBLANE - bp, d), F32)], axis=0)
    mod_all = _modulation(c_all, w_mod, row(b_mod))

    ck = cache_k.reshape(depth, n_pool, page, aw)
    cv = cache_v.reshape(depth, n_pool, page, aw)
    clf_t = jnp.swapaxes(cache_logf, 2, 3)

    sizes = (aw, aw, aw, nh, lw, lw, cw, cw, d, d, d)
    offs = [0]
    for s in sizes:
        offs.append(offs[-1] + s)
    names = ("w_q", "w_k", "w_v", "w_fl", "w_xl", "w_gl", "w_ga", "w_gb", "w_gA", "w_gB", "w_gC")

    xp = x_prompt.reshape(t, d)
    xs = x_sample.reshape(bs, d)
    outs_p, outs_s = [], []
    for l in range(depth):
        wts = {nm: w_in[l, :, offs[n]:offs[n + 1]].astype(BF16) for n, nm in enumerate(names)}
        wts["w_fl"] = jnp.pad(wts["w_fl"], ((0, 0), (0, LANE - nh)))
        wts["b_fl"] = jnp.pad(b_forget[l][None, :], ((0, 0), (0, LANE - nh)))
        wts.update(
            g_mix=norm_mix[l][None], g_ffn=norm_ffn[l][None], g_q=q_norm[l][None], g_k=k_norm[l][None],
            w_lconv=w_lru_conv[l], b_lconv=b_lru_conv[l][None], w_rg=w_rg[l].astype(BF16), b_rg=b_rg[l][None],
            w_ig=w_ig[l].astype(BF16), b_ig=b_ig[l][None], lru_L=lru_L[l][None],
            w_cconv=w_conf_conv[l], b_cconv=b_conf_conv[l][None], cln_g=conf_ln_g[l][None], cln_b=conf_ln_b[l][None],
            w_o_attn=w_o_attn[l].astype(BF16), w_o_lru=w_o_lru[l].astype(BF16), w_o_conf=w_o_conf[l].astype(BF16),
            w_out=w_out[l].astype(BF16), w_ua=w_up[l, :, :dff].astype(BF16), w_uv=w_up[l, :, dff:].astype(BF16),
            w_fconv=w_ffn_conv[l], b_fconv=b_ffn_conv[l][None], w_down=w_down[l].astype(BF16))
        state = dict(layer=l, cache_k=ck, cache_v=cv, cache_lf_t=clf_t, page_table=page_table,
                     h0=state_lru_h[l], lbuf_t=jnp.swapaxes(state_lru_conv[l], 0, 1),
                     cbuf_t=jnp.swapaxes(state_conf_conv[l], 0, 1), fbuf_t=jnp.swapaxes(state_ffn_conv[l], 0, 1))
        xp, op = _layer(xp, mod_all[l, bs:bs + bp], wts, None)
        xs, os_ = _layer(xs, mod_all[l, :bs], wts, state)
        outs_p.append(op)
        outs_s.append(os_)

    stack = lambda outs, fn: jnp.stack([fn(o) for o in outs])
    lcw, ccw, fcw = w_lru_conv.shape[1], w_conf_conv.shape[1], w_ffn_conv.shape[1]
    y_p = xp.reshape(bp, t, d)
    y_s = xs.reshape(bs, 1, d)
    k_p = stack(outs_p, lambda o: o["k"].reshape(bp, t, nh, hd))
    v_p = stack(outs_p, lambda o: o["v"].reshape(bp, t, nh, hd))
    lf_p = stack(outs_p, lambda o: o["lf"].reshape(bp, t, nh))
    h_p = stack(outs_p, lambda o: o["h"])
    lb_p = stack(outs_p, lambda o: o["xl"][None, t - (lcw - 1):])
    cb_p = stack(outs_p, lambda o: o["u"][None, t - (ccw - 1):])
    fb_p = stack(outs_p, lambda o: o["ua"][None, SUBLANE - (fcw - 1):])
    k_s = stack(outs_s, lambda o: o["k"].reshape(bs, 1, nh, hd))
    v_s = stack(outs_s, lambda o: o["v"].reshape(bs, 1, nh, hd))
    lf_s = stack(outs_s, lambda o: o["lf"].reshape(bs, 1, nh))
    h_s = stack(outs_s, lambda o: o["h"])
    lb_s = jnp.stack([jnp.concatenate([state_lru_conv[l][:, 1:], outs_s[l]["xl"][:, None]], axis=1)
                      for l in range(depth)])
    cb_s = jnp.stack([jnp.concatenate([state_conf_conv[l][:, 1:], outs_s[l]["u"][:, None]], axis=1)
                      for l in range(depth)])
    fb_s = jnp.stack([jnp.concatenate([state_ffn_conv[l][:, 1:], outs_s[l]["ua"][:, None]], axis=1)
                      for l in range(depth)])
    return (y_p, y_s, k_p, v_p, lf_p, h_p, lb_p, cb_p, fb_p, k_s, v_s, lf_s, h_s, lb_s, cb_s, fb_s)
```
